```python
import jax, jax.numpy as jnp
from jax import lax
import numpy as np

D_MODEL = 1024
BATCH = 4
SEQ = 8192
DEPTH = 4

CONV_DIM = D_MODEL // 2
CONV_WIDTH = 31
POOL_WINDOWS = (2, 4, 8, 16)
N_POOL_GROUPS = len(POOL_WINDOWS)
POOL_DIM = D_MODEL // 2
POOL_GROUP_DIM = POOL_DIM // N_POOL_GROUPS
POOL_GROUP_OUT = D_MODEL // N_POOL_GROUPS
IN_SPLITS = (CONV_DIM, 2 * CONV_DIM, 2 * CONV_DIM + POOL_DIM, 2 * CONV_DIM + POOL_DIM + D_MODEL)
IN_COLS = 2 * CONV_DIM + POOL_DIM + 2 * D_MODEL
D_FF = 2816
N_EXPERTS = 8
TOP_K = 2
D_FF_EXPERT = 3584
RMS_EPS = 1e-6
LN_EPS = 1e-5

kernel_name = "hybrid_conv_pool_gated_moe_trunk"


def rms_norm(x, g):
    xf = x.astype(jnp.float32)
    y = xf * lax.rsqrt(jnp.mean(xf * xf, axis=-1, keepdims=True) + RMS_EPS)
    return (y * g.astype(jnp.float32)).astype(x.dtype)


def layer_norm(x, g, b):
    xf = x.astype(jnp.float32)
    mu = jnp.mean(xf, axis=-1, keepdims=True)
    xc = xf - mu
    var = jnp.mean(xc * xc, axis=-1, keepdims=True)
    y = xc * lax.rsqrt(var + LN_EPS) * g.astype(jnp.float32) + b.astype(jnp.float32)
    return y.astype(x.dtype)


def swiglu(h, w_gate, w_up, w_down):
    return (jax.nn.silu(h @ w_gate) * (h @ w_up)) @ w_down


def conformer_conv(a, b, dw, dw_b, ln_g, ln_b, w_conv_out):
    u = a * jax.nn.sigmoid(b)
    u = lax.conv_general_dilated(
        u, dw[:, None, :].astype(u.dtype), window_strides=(1,),
        padding=[(CONV_WIDTH - 1, 0)],
        dimension_numbers=("NWC", "WIO", "NWC"),
        feature_group_count=u.shape[-1]) + dw_b
    u = jax.nn.silu(layer_norm(u, ln_g, ln_b))
    return u @ w_conv_out


def multiscale_pool(p, w_pool, pool_scale):
    B, S, _ = p.shape
    pf = p.astype(jnp.float32).reshape(B, S, N_POOL_GROUPS, POOL_GROUP_DIM)
    cs = jnp.cumsum(pf, axis=1)
    pos = jnp.arange(S, dtype=jnp.int32)
    deltas = []
    for g, w in enumerate(POOL_WINDOWS):
        csg = cs[:, :, g]
        prev = jnp.pad(csg, ((0, 0), (w, 0), (0, 0)))[:, :S]
        count = jnp.minimum(pos + 1, w).astype(jnp.float32)[:, None]
        deltas.append((csg - prev) / count - pf[:, :, g])
    d = jnp.stack(deltas, axis=2).astype(p.dtype)
    y = jnp.einsum("bsgc,gco->bsgo", d, w_pool).reshape(B, S, D_MODEL)
    return y * pool_scale


def hybrid_mixer(h, w_in, conv_dw, conv_dw_b, conv_ln_g, conv_ln_b, w_conv_out,
                 w_pool, pool_scale, w_out):
    proj = h @ w_in
    a, b, p, gc, gp = jnp.split(proj, IN_SPLITS, axis=-1)
    yc = conformer_conv(a, b, conv_dw, conv_dw_b, conv_ln_g, conv_ln_b, w_conv_out)
    yp = multiscale_pool(p, w_pool, pool_scale)
    merged = jax.nn.sigmoid(gc) * yc + jax.nn.sigmoid(gp) * yp
    return merged @ w_out


def moe_ffn(h, router_w, w_gate, w_up, w_down):
    B, S, D = h.shape
    t = h.reshape(B * S, D)
    logits = (t @ router_w).astype(jnp.float32)
    top_v, top_i = lax.top_k(logits, TOP_K)
    top_p = jax.nn.softmax(top_v, axis=-1)
    comb = jnp.sum(jax.nn.one_hot(top_i, N_EXPERTS, dtype=jnp.float32) * top_p[..., None],
                   axis=1).astype(h.dtype)
    y = jnp.zeros_like(t)
    for e in range(N_EXPERTS):
        y = y + comb[:, e:e + 1] * swiglu(t, w_gate[e], w_up[e], w_down[e])
    return y.reshape(B, S, D)


def setup_inputs(seed: int = 0) -> dict:
    key = jax.random.key(seed)
    ks = iter(jax.random.split(key, 32))
    n_dense = (DEPTH + 1) // 2
    n_moe = DEPTH // 2
    f32 = jnp.float32

    def nrm(shape, scale):
        return jax.random.normal(next(ks), shape, f32) * scale

    def gain(shape):
        return 1.0 + 0.02 * jax.random.normal(next(ks), shape, f32)

    return {
        "x": nrm((BATCH, SEQ, D_MODEL), 1.0),
        "mix_norm": gain((DEPTH, D_MODEL)),
        "w_in": nrm((DEPTH, D_MODEL, IN_COLS), D_MODEL ** -0.5),
        "conv_dw": nrm((DEPTH, CONV_WIDTH, CONV_DIM), CONV_WIDTH ** -0.5),
        "conv_dw_b": nrm((DEPTH, CONV_DIM), 0.02),
        "conv_ln_g": gain((DEPTH, CONV_DIM)),
        "conv_ln_b": nrm((DEPTH, CONV_DIM), 0.02),
        "w_conv_out": nrm((DEPTH, CONV_DIM, D_MODEL), CONV_DIM ** -0.5),
        "w_pool": nrm((DEPTH, N_POOL_GROUPS, POOL_GROUP_DIM, POOL_GROUP_OUT), POOL_GROUP_DIM ** -0.5),
        "pool_scale": gain((DEPTH, D_MODEL)),
        "w_out": nrm((DEPTH, D_MODEL, D_MODEL), D_MODEL ** -0.5),
        "ffn_norm": gain((DEPTH, D_MODEL)),
        "dense_w_gate": nrm((n_dense, D_MODEL, D_FF), D_MODEL ** -0.5),
        "dense_w_up": nrm((n_dense, D_MODEL, D_FF), D_MODEL ** -0.5),
        "dense_w_down": nrm((n_dense, D_FF, D_MODEL), D_FF ** -0.5),
        "router_w": nrm((n_moe, D_MODEL, N_EXPERTS), D_MODEL ** -0.5),
        "moe_w_gate": nrm((n_moe, N_EXPERTS, D_MODEL, D_FF_EXPERT), D_MODEL ** -0.5),
        "moe_w_up": nrm((n_moe, N_EXPERTS, D_MODEL, D_FF_EXPERT), D_MODEL ** -0.5),
        "moe_w_down": nrm((n_moe, N_EXPERTS, D_FF_EXPERT, D_MODEL), D_FF_EXPERT ** -0.5),
        "final_norm": gain((D_MODEL,)),
    }


def reference(x, mix_norm, w_in, conv_dw, conv_dw_b, conv_ln_g, conv_ln_b, w_conv_out,
              w_pool, pool_scale, w_out, ffn_norm, dense_w_gate, dense_w_up, dense_w_down,
              router_w, moe_w_gate, moe_w_up, moe_w_down, final_norm):
    for layer in range(DEPTH):
        h = rms_norm(x, mix_norm[layer])
        x = x + hybrid_mixer(h, w_in[layer], conv_dw[layer], conv_dw_b[layer],
                             conv_ln_g[layer], conv_ln_b[layer], w_conv_out[layer],
                             w_pool[layer], pool_scale[layer], w_out[layer])
        h = rms_norm(x, ffn_norm[layer])
        i = layer // 2
        if layer % 2 == 0:
            x = x + swiglu(h, dense_w_gate[i], dense_w_up[i], dense_w_down[i])
        else:
            x = x + moe_ffn(h, router_w[i], moe_w_gate[i], moe_w_up[i], moe_w_down[i])
    return rms_norm(x, final_norm)
```

```python
import functools

import jax
import jax.numpy as jnp
from jax import lax
from jax.experimental import pallas as pl
from jax.experimental.pallas import tpu as pltpu

RMS_EPS = 1e-6
LN_EPS = 1e-5
CONV_WIDTH = 31
POOL_WINDOWS = (2, 4, 8, 16)
N_EXPERTS = 8
TOP_K = 2

V7X_SUBLANES = 8
V7X_LANES = 128
V7X_VMEM_LIMIT_BYTES = 56 * 1024 * 1024

CONV_HALO = 32
POOL_HALO = 16
CONV_ROW_BLOCK = 32

MIX_TILE = 512
DENSE_TILE = 512
DENSE_FF_CHUNK = 512
ROUTER_TILE = 1024
MOE_TILE = 1024
MOE_FF_CHUNK = 512
MOE_INDEX_SLOTS = 3

_BF16 = jnp.bfloat16
_F32 = jnp.float32


def _rms_norm(x, g):
    return x * lax.rsqrt(jnp.mean(x * x, axis=-1, keepdims=True) + RMS_EPS) * g


def _const_spec(shape):
    return pl.BlockSpec(shape, lambda *_: (0,) * len(shape), pipeline_mode=pl.Buffered(1))


def _mixer_kernel(has_moe_in, *refs):
    if has_moe_in:
        x_ref, y0_ref, y1_ref = refs[:3]
        refs = refs[3:]
    else:
        x_ref = refs[0]
        refs = refs[1:]
    (g_ref, w_in_ref, dw_ref, dwb_ref, lng_ref, lnb_ref, wco_ref, wpool_ref, pscale_ref,
     wout_ref, o_ref, ubuf, pbuf, cbuf) = refs
    T = x_ref.shape[0]
    C = ubuf.shape[1]
    D = x_ref.shape[1]
    s = pl.program_id(1)

    @pl.when(s == 0)
    def _():
        ubuf[0:CONV_HALO, :] = jnp.zeros((CONV_HALO, C), _F32)
        pbuf[0:POOL_HALO, :] = jnp.zeros((POOL_HALO, C), _F32)

    x = x_ref[...]
    if has_moe_in:
        x = x + y0_ref[...] + y1_ref[...]
    hb = _rms_norm(x, g_ref[...]).astype(_BF16)

    def proj(c0, width):
        return jnp.dot(hb, w_in_ref[:, c0:c0 + width], preferred_element_type=_F32)

    u = proj(0, C) * jax.nn.sigmoid(proj(C, C))
    ubuf[CONV_HALO:CONV_HALO + T, :] = u
    for rb in range(T // CONV_ROW_BLOCK):
        r0 = rb * CONV_ROW_BLOCK
        acc = jnp.zeros((CONV_ROW_BLOCK, C), _F32)
        for k in range(CONV_WIDTH):
            off = CONV_HALO - (CONV_WIDTH - 1) + k + r0
            acc = acc + dw_ref[k:k + 1, :] * ubuf[off:off + CONV_ROW_BLOCK, :]
        cbuf[r0:r0 + CONV_ROW_BLOCK, :] = acc
    ubuf[0:CONV_HALO, :] = ubuf[T:T + CONV_HALO, :]
    c = cbuf[...] + dwb_ref[...]
    mu = jnp.mean(c, axis=-1, keepdims=True)
    cc = c - mu
    var = jnp.mean(cc * cc, axis=-1, keepdims=True)
    cn = cc * lax.rsqrt(var + LN_EPS) * lng_ref[...] + lnb_ref[...]
    cn = cn * jax.nn.sigmoid(cn)
    yc = jnp.dot(cn.astype(_BF16), wco_ref[...], preferred_element_type=_F32)

    p = proj(2 * C, C)
    pbuf[POOL_HALO:POOL_HALO + T, :] = p
    pos = s * T + lax.broadcasted_iota(jnp.int32, (T, 1), 0)
    gw = C // len(POOL_WINDOWS)
    go = D // len(POOL_WINDOWS)
    yps = []
    for g, w in enumerate(POOL_WINDOWS):
        cur = pbuf[POOL_HALO:POOL_HALO + T, g * gw:(g + 1) * gw]
        win = cur
        for j in range(1, w):
            win = win + pbuf[POOL_HALO - j:POOL_HALO - j + T, g * gw:(g + 1) * gw]
        count = jnp.minimum(pos + 1, w).astype(_F32)
        delta = win / count - cur
        yps.append(jnp.dot(delta.astype(_BF16), wpool_ref[g], preferred_element_type=_F32))
    pbuf[0:POOL_HALO, :] = pbuf[T:T + POOL_HALO, :]
    yp = jnp.concatenate(yps, axis=-1) * pscale_ref[...]
    assert yp.shape == (T, len(POOL_WINDOWS) * go)

    merged = (jax.nn.sigmoid(proj(3 * C, D)) * yc + jax.nn.sigmoid(proj(3 * C + D, D)) * yp)
    o_ref[...] = x + jnp.dot(merged.astype(_BF16), wout_ref[...], preferred_element_type=_F32)


def _mixer(x, moe_in, g, w_in, dw, dwb, lng, lnb, wco, wpool, pscale, wout):
    B, S, D = x.shape
    C = dw.shape[1]
    T = MIX_TILE
    assert S % T == 0 and T % CONV_ROW_BLOCK == 0 and T >= CONV_HALO
    assert CONV_HALO >= CONV_WIDTH - 1 and POOL_HALO >= max(POOL_WINDOWS) - 1
    assert w_in.shape == (D, 3 * C + 2 * D)
    row_spec = pl.BlockSpec((None, T, D), lambda b, s: (b, s, 0))
    has_moe_in = moe_in is not None
    acts = (x,) + (tuple(moe_in) if has_moe_in else ())
    params = (g.reshape(1, D), w_in, dw, dwb.reshape(1, C), lng.reshape(1, C), lnb.reshape(1, C),
              wco, wpool, pscale.reshape(1, D), wout)
    return pl.pallas_call(
        functools.partial(_mixer_kernel, has_moe_in),
        grid=(B, S // T),
        in_specs=[row_spec] * len(acts) + [_const_spec(p.shape) for p in params],
        out_specs=row_spec,
        out_shape=jax.ShapeDtypeStruct((B, S, D), _F32),
        scratch_shapes=[
            pltpu.VMEM((CONV_HALO + T, C), _F32),
            pltpu.VMEM((POOL_HALO + T, C), _F32),
            pltpu.VMEM((T, C), _F32),
        ],
        compiler_params=pltpu.CompilerParams(
            dimension_semantics=("arbitrary", "arbitrary"),
            vmem_limit_bytes=V7X_VMEM_LIMIT_BYTES),
        name="mixer",
    )(*acts, *params)


def _ff_chunks(total, chunk):
    return [(c0, min(chunk, total - c0)) for c0 in range(0, total, chunk)]


def _dense_ffn_kernel(x_ref, g_ref, wg_ref, wu_ref, wd_ref, o_ref):
    x = x_ref[...]
    hb = _rms_norm(x, g_ref[...]).astype(_BF16)
    acc = x
    for c0, cw in _ff_chunks(wg_ref.shape[1], DENSE_FF_CHUNK):
        gate = jnp.dot(hb, wg_ref[:, c0:c0 + cw], preferred_element_type=_F32)
        up = jnp.dot(hb, wu_ref[:, c0:c0 + cw], preferred_element_type=_F32)
        act = (gate * jax.nn.sigmoid(gate) * up).astype(_BF16)
        acc = acc + jnp.dot(act, wd_ref[c0:c0 + cw, :], preferred_element_type=_F32)
    o_ref[...] = acc


def _dense_ffn(x2, g, wg, wu, wd):
    N, D = x2.shape
    TM = DENSE_TILE
    assert N % TM == 0
    row_spec = pl.BlockSpec((TM, D), lambda i: (i, 0))
    params = (g.reshape(1, D), wg, wu, wd)
    return pl.pallas_call(
        _dense_ffn_kernel,
        grid=(N // TM,),
        in_specs=[row_spec] + [_const_spec(p.shape) for p in params],
        out_specs=row_spec,
        out_shape=jax.ShapeDtypeStruct((N, D), _F32),
        compiler_params=pltpu.CompilerParams(
            dimension_semantics=("arbitrary",), vmem_limit_bytes=V7X_VMEM_LIMIT_BYTES),
        name="dense_ffn",
    )(x2, *params)


def _router_kernel(x_ref, g_ref, rw_ref, tri_ref, idx_ref, wts_ref, rank_ref, cnt_ref, carry):
    i = pl.program_id(0)

    @pl.when(i == 0)
    def _():
        carry[...] = jnp.zeros_like(carry)

    h = _rms_norm(x_ref[...], g_ref[...])
    logits = lax.dot_general(rw_ref[...], h, (((1,), (1,)), ((), ())),
                             precision=lax.Precision.HIGHEST,
                             preferred_element_type=_F32)
    E, TR = logits.shape
    eid = lax.broadcasted_iota(jnp.int32, (E, TR), 0)
    m1 = jnp.max(logits, axis=0, keepdims=True)
    i1 = jnp.min(jnp.where(logits == m1, eid, E), axis=0, keepdims=True)
    rest = jnp.where(eid == i1, -jnp.inf, logits)
    m2 = jnp.max(rest, axis=0, keepdims=True)
    i2 = jnp.min(jnp.where(rest == m2, eid, E), axis=0, keepdims=True)
    e2 = jnp.exp(m2 - m1)
    denom = 1.0 + e2
    idx_ref[...] = jnp.concatenate([i1, i2], axis=0)
    wts_ref[...] = jnp.concatenate([1.0 / denom, e2 / denom], axis=0)

    sel1 = eid == i1
    sel2 = eid == i2
    onehot = jnp.where(sel1 | sel2, 1.0, 0.0)
    before = jnp.dot(onehot.astype(_BF16), tri_ref[...], preferred_element_type=_F32) + carry[...]
    r1 = jnp.sum(jnp.where(sel1, before, 0.0), axis=0, keepdims=True)
    r2 = jnp.sum(jnp.where(sel2, before, 0.0), axis=0, keepdims=True)
    rank_ref[...] = jnp.concatenate([r1, r2], axis=0).astype(jnp.int32)
    total = carry[...] + jnp.sum(onehot, axis=1, keepdims=True)
    carry[...] = total
    cnt_ref[...] = jnp.broadcast_to(total, cnt_ref.shape)


def _router(x2, g, router_w):
    N, D = x2.shape
    E = router_w.shape[1]
    TR = ROUTER_TILE
    assert N % TR == 0 and E == N_EXPERTS
    tri = (lax.broadcasted_iota(jnp.int32, (TR, TR), 0)
           < lax.broadcasted_iota(jnp.int32, (TR, TR), 1)).astype(_BF16)
    lane_spec = pl.BlockSpec((TOP_K, TR), lambda i: (0, i))
    idx, wts, rank, cnt = pl.pallas_call(
        _router_kernel,
        grid=(N // TR,),
        in_specs=[pl.BlockSpec((TR, D), lambda i: (i, 0)),
                  _const_spec((1, D)), _const_spec((E, D)), _const_spec((TR, TR))],
        out_specs=[lane_spec, lane_spec, lane_spec,
                   pl.BlockSpec((E, V7X_LANES), lambda i: (0, 0))],
        out_shape=[jax.ShapeDtypeStruct((TOP_K, N), jnp.int32),
                   jax.ShapeDtypeStruct((TOP_K, N), _F32),
                   jax.ShapeDtypeStruct((TOP_K, N), jnp.int32),
                   jax.ShapeDtypeStruct((E, V7X_LANES), _F32)],
        scratch_shapes=[pltpu.VMEM((E, 1), _F32)],
        compiler_params=pltpu.CompilerParams(
            dimension_semantics=("arbitrary",), vmem_limit_bytes=V7X_VMEM_LIMIT_BYTES),
        name="router",
    )(x2, g.reshape(1, D), router_w.T, tri)
    return idx, wts, rank, cnt[:, 0].astype(jnp.int32)


def _moe_kernel(tile_expert, tile_rows,
                dest_hbm, x_hbm, roww_ref, g_ref, wg_ref, wu_ref, wd_ref,
                y_hbm,
                idx_smem, xbuf, xb, acc, obuf, idx_sem, gather_sem, scatter_sem):
    i = pl.program_id(0)
    j = pl.program_id(1)
    n_tiles = pl.num_programs(0)
    n_chunks = pl.num_programs(1)
    N = x_hbm.shape[0]
    TM = xbuf.shape[1]

    def idx_copy(tile):
        slot = tile % MOE_INDEX_SLOTS
        return pltpu.make_async_copy(dest_hbm.at[tile], idx_smem.at[slot], idx_sem.at[slot])

    def gather_copy(tile, r):
        slot = tile % 2
        token = idx_smem[tile % MOE_INDEX_SLOTS, r] & (N - 1)
        return pltpu.make_async_copy(x_hbm.at[token], xbuf.at[slot, r], gather_sem.at[slot])

    def scatter_copy(tile, r):
        slot = tile % 2
        dest = idx_smem[tile % MOE_INDEX_SLOTS, r]
        return pltpu.make_async_copy(obuf.at[slot, r], y_hbm.at[dest], scatter_sem.at[slot])

    def for_rows(tile, fn):
        def body(r, carry):
            fn(tile, r)
            return carry
        lax.fori_loop(0, tile_rows[tile], body, 0)

    def start_gather(tile, r):
        gather_copy(tile, r).start()

    def wait_gather(tile, r):
        slot = tile % 2
        pltpu.make_async_copy(x_hbm.at[0], xbuf.at[slot, r], gather_sem.at[slot]).wait()

    def start_scatter(tile, r):
        scatter_copy(tile, r).start()

    def wait_scatter(tile, r):
        slot = tile % 2
        pltpu.make_async_copy(obuf.at[slot, r], y_hbm.at[0], scatter_sem.at[slot]).wait()

    @pl.when(j == 0)
    def _():
        @pl.when(i == 0)
        def _():
            xbuf[...] = jnp.zeros_like(xbuf)
            idx_copy(0).start()
            idx_copy(0).wait()
            for_rows(0, start_gather)
            if_next = n_tiles > 1

            @pl.when(if_next)
            def _():
                idx_copy(1).start()

        for_rows(i, wait_gather)
        xb[...] = _rms_norm(xbuf[i % 2], g_ref[...]).astype(_BF16)
        acc[...] = jnp.zeros_like(acc)

        @pl.when(i + 1 < n_tiles)
        def _():
            idx_copy(i + 1).wait()
            for_rows(i + 1, start_gather)

        @pl.when(i + 2 < n_tiles)
        def _():
            idx_copy(i + 2).start()

    @pl.when(tile_rows[i] > 0)
    def _():
        h = xb[...]
        gate = jnp.dot(h, wg_ref[...], preferred_element_type=_F32)
        up = jnp.dot(h, wu_ref[...], preferred_element_type=_F32)
        act = (gate * jax.nn.sigmoid(gate) * up).astype(_BF16)
        acc[...] += jnp.dot(act, wd_ref[...], preferred_element_type=_F32)

    @pl.when(j == n_chunks - 1)
    def _():
        @pl.when(i >= 2)
        def _():
            for_rows(i - 2, wait_scatter)

        obuf[i % 2] = acc[...] * roww_ref[...]
        for_rows(i, start_scatter)

        @pl.when(i == n_tiles - 1)
        def _():
            @pl.when(i >= 1)
            def _():
                for_rows(i - 1, wait_scatter)
            for_rows(i, wait_scatter)


def _moe_ffn(x2, g, router_w, wg, wu, wd):
    N, D = x2.shape
    E, _, F = wg.shape
    TM, FC = MOE_TILE, MOE_FF_CHUNK
    assert N & (N - 1) == 0, "token index is recovered from the destination id by masking"
    assert (TOP_K * N) % TM == 0 and F % FC == 0
    n_tiles = TOP_K * N // TM + E
    P = n_tiles * TM

    idx, wts, rank, cnt = _router(x2, g, router_w)

    tiles_per = (cnt + TM - 1) // TM
    tile_end = jnp.cumsum(tiles_per)
    tile_start = tile_end - tiles_per
    pos = (tile_start * TM)[idx.reshape(-1)] + rank.reshape(-1)
    dest = jnp.zeros((P,), jnp.int32).at[pos].set(jnp.arange(TOP_K * N, dtype=jnp.int32))
    roww = jnp.zeros((P,), _F32).at[pos].set(wts.reshape(-1))
    tile_id = jnp.arange(n_tiles, dtype=jnp.int32)
    used = tile_id < tile_end[-1]
    owner = jnp.minimum(jnp.searchsorted(tile_end, tile_id, side="right"), E - 1).astype(jnp.int32)
    last_owner = owner[jnp.maximum(tile_end[-1] - 1, 0)]
    tile_expert = jnp.where(used, owner, last_owner).astype(jnp.int32)
    tile_rows = jnp.where(used, jnp.clip(cnt[owner] - (tile_id - tile_start[owner]) * TM, 0, TM),
                          0).astype(jnp.int32)

    n_chunks = F // FC

    def chunk_of(i, j, te, tr):
        return jnp.where(tr[i] > 0, j, n_chunks - 1)

    grid_spec = pltpu.PrefetchScalarGridSpec(
        num_scalar_prefetch=2,
        grid=(n_tiles, n_chunks),
        in_specs=[
            pl.BlockSpec(memory_space=pl.ANY),
            pl.BlockSpec(memory_space=pl.ANY),
            pl.BlockSpec((TM, 1), lambda i, j, te, tr: (i, 0)),
            pl.BlockSpec((1, D), lambda i, j, te, tr: (0, 0)),
            pl.BlockSpec((None, D, FC), lambda i, j, te, tr: (te[i], 0, chunk_of(i, j, te, tr))),
            pl.BlockSpec((None, D, FC), lambda i, j, te, tr: (te[i], 0, chunk_of(i, j, te, tr))),
            pl.BlockSpec((None, FC, D), lambda i, j, te, tr: (te[i], chunk_of(i, j, te, tr), 0)),
        ],
        out_specs=pl.BlockSpec(memory_space=pl.ANY),
        scratch_shapes=[
            pltpu.SMEM((MOE_INDEX_SLOTS, TM), jnp.int32),
            pltpu.VMEM((2, TM, D), _F32),
            pltpu.VMEM((TM, D), _BF16),
            pltpu.VMEM((TM, D), _F32),
            pltpu.VMEM((2, TM, D), _F32),
            pltpu.SemaphoreType.DMA((MOE_INDEX_SLOTS,)),
            pltpu.SemaphoreType.DMA((2,)),
            pltpu.SemaphoreType.DMA((2,)),
        ],
    )
    y = pl.pallas_call(
        _moe_kernel,
        grid_spec=grid_spec,
        out_shape=jax.ShapeDtypeStruct((TOP_K * N, D), _F32),
        compiler_params=pltpu.CompilerParams(
            dimension_semantics=("arbitrary", "arbitrary"),
            vmem_limit_bytes=V7X_VMEM_LIMIT_BYTES),
        name="moe_ffn",
    )(tile_expert, tile_rows, dest.reshape(n_tiles, TM), x2, roww.reshape(P, 1),
      g.reshape(1, D), wg, wu, wd)
    return y.reshape(TOP_K, N, D)


def _final_norm_kernel(x_ref, y0_ref, y1_ref, g_ref, o_ref):
    o_ref[...] = _rms_norm(x_ref[...] + y0_ref[...] + y1_ref[...], g_ref[...])


def _final_norm(x2, y, g):
    N, D = x2.shape
    TM = DENSE_TILE
    row_spec = pl.BlockSpec((TM, D), lambda i: (i, 0))
    return pl.pallas_call(
        _final_norm_kernel,
        grid=(N // TM,),
        in_specs=[row_spec, row_spec, row_spec, _const_spec((1, D))],
        out_specs=row_spec,
        out_shape=jax.ShapeDtypeStruct((N, D), _F32),
        compiler_params=pltpu.CompilerParams(dimension_semantics=("arbitrary",)),
        name="final_norm",
    )(x2, y[0], y[1], g.reshape(1, D))


def kernel(x, mix_norm, w_in, conv_dw, conv_dw_b, conv_ln_g, conv_ln_b, w_conv_out, w_pool,
           pool_scale, w_out, ffn_norm, dense_w_gate, dense_w_up, dense_w_down, router_w,
           moe_w_gate, moe_w_up, moe_w_down, final_norm):
    B, S, D = x.shape
    depth = mix_norm.shape[0]
    bf = lambda w: w.astype(_BF16)
    moe_out = None
    for layer in range(depth):
        moe_in = None if moe_out is None else tuple(m.reshape(B, S, D) for m in moe_out)
        x = _mixer(x, moe_in, mix_norm[layer], bf(w_in[layer]), conv_dw[layer], conv_dw_b[layer],
                   conv_ln_g[layer], conv_ln_b[layer], bf(w_conv_out[layer]), bf(w_pool[layer]),
                   pool_scale[layer], bf(w_out[layer]))
        moe_out = None
        x2 = x.reshape(B * S, D)
        i = layer // 2
        if layer % 2 == 0:
            x = _dense_ffn(x2, ffn_norm[layer], bf(dense_w_gate[i]), bf(dense_w_up[i]),
                           bf(dense_w_down[i])).reshape(B, S, D)
        else:
            moe_out = _moe_ffn(x2, ffn_norm[layer], router_w[i], bf(moe_w_gate[i]),
                               bf(moe_w_up[i]), bf(moe_w_down[i]))
    x2 = x.reshape(B * S, D)
    if moe_out is None:
        moe_out = jnp.zeros((TOP_K,) + x2.shape, _F32)
    return _final_norm(x2, moe_out, final_norm).reshape(B, S, D)
```

```python
import functools

import jax
import jax.numpy as jnp
from jax import lax
from jax.experimental import pallas as pl
from jax.experimental.pallas import tpu as pltpu

RMS_EPS = 1e-6
LN_EPS = 1e-5
CONV_WIDTH = 31
POOL_WINDOWS = (2, 4, 8, 16)
N_EXPERTS = 8
TOP_K = 2

V7X_SUBLANES = 8
V7X_LANES = 128
V7X_VMEM_LIMIT_BYTES = 56 * 1024 * 1024

CONV_HALO = 32
POOL_HALO = 16
ROW_INTERLEAVE = 4
ROW_SPAN = V7X_SUBLANES * ROW_INTERLEAVE
CONV_ROW_BLOCK = 128

MIX_TILE = 512
DENSE_TILE = 512
DENSE_FF_CHUNK = 512
ROUTER_TILE = 1024
MOE_TILE = 1024
MOE_FF_CHUNK = 512

_BF16 = jnp.bfloat16
_F32 = jnp.float32


def _rms_norm(x, g):
    return x * lax.rsqrt(jnp.mean(x * x, axis=-1, keepdims=True) + RMS_EPS) * g


def _const_spec(shape, index=None):
    index = (0,) * len(shape) if index is None else index
    return pl.BlockSpec(shape, lambda *_: index, pipeline_mode=pl.Buffered(1))


def _layer_spec(stacked, layer):
    return _const_spec((None,) + stacked.shape[1:], (layer,) + (0,) * (stacked.ndim - 1))


def _interleaved_rows(start):
    return pl.ds(start, V7X_SUBLANES, stride=ROW_INTERLEAVE)


def _interleaved_starts(r0, rows):
    return [r0 + b + o for b in range(0, rows, ROW_SPAN) for o in range(ROW_INTERLEAVE)]


def _mixer_kernel(has_moe_in, *refs):
    if has_moe_in:
        x_ref, y0_ref, y1_ref, yw_ref = refs[:4]
        refs = refs[4:]
    else:
        x_ref = refs[0]
        refs = refs[1:]
    (g_ref, w_in_ref, dw_ref, dwb_ref, lng_ref, lnb_ref, wco_ref, wpool_ref, pscale_ref,
     wout_ref, o_ref, ubuf, pbuf, cbuf, wbuf) = refs
    T, D = x_ref.shape
    n_slab = ubuf.shape[0]
    C = n_slab * V7X_LANES
    s = pl.program_id(1)

    @pl.when(s == 0)
    def _():
        ubuf[:, 0:CONV_HALO, :] = jnp.zeros((n_slab, CONV_HALO, V7X_LANES), _F32)
        pbuf[:, 0:POOL_HALO, :] = jnp.zeros((n_slab, POOL_HALO, V7X_LANES), _F32)

    x = x_ref[...]
    if has_moe_in:
        yw = yw_ref[...]
        x = x + yw[:, 0:1] * y0_ref[...] + yw[:, 1:2] * y1_ref[...]
    hb = _rms_norm(x, g_ref[...]).astype(_BF16)

    def proj(c0, width):
        return jnp.dot(hb, w_in_ref[:, c0:c0 + width], preferred_element_type=_F32)

    def lanes(c):
        return slice(c * V7X_LANES, (c + 1) * V7X_LANES)

    u = proj(0, C) * jax.nn.sigmoid(proj(C, C))
    for c in range(n_slab):
        ubuf[c, CONV_HALO:CONV_HALO + T, :] = u[:, lanes(c)]
    p = proj(2 * C, C)
    zc = proj(3 * C, D)
    zp = proj(3 * C + D, D)
    for c in range(n_slab):
        for r0 in range(0, T, CONV_ROW_BLOCK):
            starts = _interleaved_starts(r0, CONV_ROW_BLOCK)
            accs = [None] * len(starts)
            for k in range(CONV_WIDTH):
                tap = dw_ref[k, :, lanes(c)]
                shift = CONV_HALO - (CONV_WIDTH - 1) + k
                for q, r in enumerate(starts):
                    term = tap * ubuf[c, _interleaved_rows(shift + r), :]
                    accs[q] = term if accs[q] is None else accs[q] + term
            for q, r in enumerate(starts):
                cbuf[c, _interleaved_rows(r), :] = accs[q]
    ubuf[:, 0:CONV_HALO, :] = ubuf[:, T:T + CONV_HALO, :]
    cv = jnp.concatenate([cbuf[c] for c in range(n_slab)], axis=-1) + dwb_ref[...]
    mu = jnp.mean(cv, axis=-1, keepdims=True)
    cc = cv - mu
    var = jnp.mean(cc * cc, axis=-1, keepdims=True)
    cn = cc * lax.rsqrt(var + LN_EPS) * lng_ref[...] + lnb_ref[...]
    cn = cn * jax.nn.sigmoid(cn)
    yc = jnp.dot(cn.astype(_BF16), wco_ref[...], preferred_element_type=_F32)

    for c in range(n_slab):
        pbuf[c, POOL_HALO:POOL_HALO + T, :] = p[:, lanes(c)]
    for c, w in enumerate(POOL_WINDOWS):
        for r in _interleaved_starts(0, T):
            win = pbuf[c, _interleaved_rows(POOL_HALO + r), :]
            for jj in range(1, w):
                win = win + pbuf[c, _interleaved_rows(POOL_HALO + r - jj), :]
            wbuf[c, _interleaved_rows(r), :] = win
    pbuf[:, 0:POOL_HALO, :] = pbuf[:, T:T + POOL_HALO, :]
    pos = s * T + lax.broadcasted_iota(jnp.int32, (T, 1), 0)
    yps = []
    for c, w in enumerate(POOL_WINDOWS):
        count = jnp.minimum(pos + 1, w).astype(_F32)
        delta = wbuf[c] / count - p[:, lanes(c)]
        yps.append(jnp.dot(delta.astype(_BF16), wpool_ref[c], preferred_element_type=_F32))
    yp = jnp.concatenate(yps, axis=-1) * pscale_ref[...]

    merged = jax.nn.sigmoid(zc) * yc + jax.nn.sigmoid(zp) * yp
    o_ref[...] = x + jnp.dot(merged.astype(_BF16), wout_ref[...], preferred_element_type=_F32)


def _mixer(x, moe_in, layer, g, w_in, dw8, dwb, lng, lnb, wco, wpool, pscale, wout):
    B, S, D = x.shape
    C = dw8.shape[-1]
    T = MIX_TILE
    n_slab = C // V7X_LANES
    assert S % T == 0 and T % CONV_ROW_BLOCK == 0 and CONV_ROW_BLOCK % ROW_SPAN == 0
    assert CONV_HALO >= CONV_WIDTH - 1 and POOL_HALO >= max(POOL_WINDOWS) - 1 and T >= CONV_HALO
    assert n_slab == len(POOL_WINDOWS) and D % len(POOL_WINDOWS) == 0
    assert w_in.shape[1:] == (D, 3 * C + 2 * D) and wpool.shape[1:] == (n_slab, V7X_LANES, D // n_slab)
    row_spec = pl.BlockSpec((None, T, D), lambda b, s: (b, s, 0))
    has_moe_in = moe_in is not None
    acts, act_specs = [x], [row_spec]
    if has_moe_in:
        y, yw = moe_in
        acts += [y, y, yw]
        act_specs += [pl.BlockSpec((None, None, T, D), lambda b, s: (0, b, s, 0)),
                      pl.BlockSpec((None, None, T, D), lambda b, s: (1, b, s, 0)),
                      pl.BlockSpec((None, T, TOP_K), lambda b, s: (b, s, 0))]
    params = (g, w_in, dw8, dwb, lng, lnb, wco, wpool, pscale, wout)
    return pl.pallas_call(
        functools.partial(_mixer_kernel, has_moe_in),
        grid=(B, S // T),
        in_specs=act_specs + [_layer_spec(p, layer) for p in params],
        out_specs=row_spec,
        out_shape=jax.ShapeDtypeStruct((B, S, D), _F32),
        scratch_shapes=[
            pltpu.VMEM((n_slab, CONV_HALO + T, V7X_LANES), _F32),
            pltpu.VMEM((n_slab, POOL_HALO + T, V7X_LANES), _F32),
            pltpu.VMEM((n_slab, T, V7X_LANES), _F32),
            pltpu.VMEM((n_slab, T, V7X_LANES), _F32),
        ],
        compiler_params=pltpu.CompilerParams(
            dimension_semantics=("arbitrary", "arbitrary"),
            vmem_limit_bytes=V7X_VMEM_LIMIT_BYTES),
        name="mixer",
    )(*acts, *params)


def _ff_chunks(total, chunk):
    return [(c0, min(chunk, total - c0)) for c0 in range(0, total, chunk)]


def _dense_ffn_kernel(x_ref, g_ref, wg_ref, wu_ref, wd_ref, o_ref):
    x = x_ref[...]
    hb = _rms_norm(x, g_ref[...]).astype(_BF16)
    acc = x
    for c0, cw in _ff_chunks(wg_ref.shape[1], DENSE_FF_CHUNK):
        gate = jnp.dot(hb, wg_ref[:, c0:c0 + cw], preferred_element_type=_F32)
        up = jnp.dot(hb, wu_ref[:, c0:c0 + cw], preferred_element_type=_F32)
        act = (gate * jax.nn.sigmoid(gate) * up).astype(_BF16)
        acc = acc + jnp.dot(act, wd_ref[c0:c0 + cw, :], preferred_element_type=_F32)
    o_ref[...] = acc


def _dense_ffn(x2, layer, index, g, wg, wu, wd):
    N, D = x2.shape
    TM = DENSE_TILE
    assert N % TM == 0
    row_spec = pl.BlockSpec((TM, D), lambda i: (i, 0))
    return pl.pallas_call(
        _dense_ffn_kernel,
        grid=(N // TM,),
        in_specs=[row_spec, _layer_spec(g, layer)] + [_layer_spec(w, index) for w in (wg, wu, wd)],
        out_specs=row_spec,
        out_shape=jax.ShapeDtypeStruct((N, D), _F32),
        compiler_params=pltpu.CompilerParams(
            dimension_semantics=("arbitrary",), vmem_limit_bytes=V7X_VMEM_LIMIT_BYTES),
        name="dense_ffn",
    )(x2, g, wg, wu, wd)


def _router_kernel(x_ref, g_ref, rw_ref, tri_ref, idx_ref, wts_ref, rank_ref, cnt_ref, carry):
    i = pl.program_id(0)

    @pl.when(i == 0)
    def _():
        carry[...] = jnp.zeros_like(carry)

    h = _rms_norm(x_ref[...], g_ref[...])
    logits = lax.dot_general(rw_ref[...], h, (((1,), (1,)), ((), ())),
                             precision=lax.Precision.HIGHEST,
                             preferred_element_type=_F32)
    E, TR = logits.shape
    eid = lax.broadcasted_iota(jnp.int32, (E, TR), 0)
    m1 = jnp.max(logits, axis=0, keepdims=True)
    i1 = jnp.min(jnp.where(logits == m1, eid, E), axis=0, keepdims=True)
    rest = jnp.where(eid == i1, -jnp.inf, logits)
    m2 = jnp.max(rest, axis=0, keepdims=True)
    i2 = jnp.min(jnp.where(rest == m2, eid, E), axis=0, keepdims=True)
    e2 = jnp.exp(m2 - m1)
    denom = 1.0 + e2
    idx_ref[...] = jnp.concatenate([i1, i2], axis=0)
    wts_ref[...] = jnp.concatenate([1.0 / denom, e2 / denom], axis=0)

    sel1 = eid == i1
    sel2 = eid == i2
    onehot = jnp.where(sel1 | sel2, 1.0, 0.0)
    before = jnp.dot(onehot.astype(_BF16), tri_ref[...], preferred_element_type=_F32) + carry[...]
    r1 = jnp.sum(jnp.where(sel1, before, 0.0), axis=0, keepdims=True)
    r2 = jnp.sum(jnp.where(sel2, before, 0.0), axis=0, keepdims=True)
    rank_ref[...] = jnp.concatenate([r1, r2], axis=0).astype(jnp.int32)
    total = carry[...] + jnp.sum(onehot, axis=1, keepdims=True)
    carry[...] = total
    cnt_ref[...] = jnp.broadcast_to(total, cnt_ref.shape)


def _router(x2, layer, g, router_w_t):
    N, D = x2.shape
    E = router_w_t.shape[0]
    TR = ROUTER_TILE
    assert N % TR == 0 and E == N_EXPERTS
    tri = (lax.broadcasted_iota(jnp.int32, (TR, TR), 0)
           < lax.broadcasted_iota(jnp.int32, (TR, TR), 1)).astype(_BF16)
    lane_spec = pl.BlockSpec((TOP_K, TR), lambda i: (0, i))
    idx, wts, rank, cnt = pl.pallas_call(
        _router_kernel,
        grid=(N // TR,),
        in_specs=[pl.BlockSpec((TR, D), lambda i: (i, 0)),
                  _layer_spec(g, layer), _const_spec((E, D)), _const_spec((TR, TR))],
        out_specs=[lane_spec, lane_spec, lane_spec,
                   pl.BlockSpec((E, V7X_LANES), lambda i: (0, 0))],
        out_shape=[jax.ShapeDtypeStruct((TOP_K, N), jnp.int32),
                   jax.ShapeDtypeStruct((TOP_K, N), _F32),
                   jax.ShapeDtypeStruct((TOP_K, N), jnp.int32),
                   jax.ShapeDtypeStruct((E, V7X_LANES), _F32)],
        scratch_shapes=[pltpu.VMEM((E, 1), _F32)],
        compiler_params=pltpu.CompilerParams(
            dimension_semantics=("arbitrary",), vmem_limit_bytes=V7X_VMEM_LIMIT_BYTES),
        name="router",
    )(x2, g, router_w_t, tri)
    return idx, wts, rank, cnt[:, 0].astype(jnp.int32)


def _moe_kernel(tile_expert, tile_rows,
                dest_hbm, x_hbm, g_ref, wg_ref, wu_ref, wd_ref,
                y_hbm,
                idx_smem0, idx_smem1, xbuf, xb, acc, obuf, idx_sem, gather_sem, scatter_sem):
    i = pl.program_id(0)
    j = pl.program_id(1)
    n_tiles = pl.num_programs(0)
    n_chunks = pl.num_programs(1)
    N = x_hbm.shape[0] * V7X_SUBLANES
    TM, D = acc.shape
    idx_smem = (idx_smem0, idx_smem1)

    def idx_copy(tile, slot):
        return pltpu.make_async_copy(dest_hbm.at[tile], idx_smem[slot], idx_sem.at[slot])

    def for_rows(n, fn):
        def group(gi, carry):
            base = gi * V7X_SUBLANES
            for k in range(V7X_SUBLANES):
                fn(base + k, gi, k)
            return carry

        def single(r, carry):
            fn(r, lax.shift_right_logical(r, 3), r & (V7X_SUBLANES - 1))
            return carry

        n_groups = lax.shift_right_logical(n, 3)
        lax.fori_loop(0, n_groups, group, 0)
        lax.fori_loop(n_groups * V7X_SUBLANES, n, single, 0)

    def start_gathers(tile, slot):
        def start(r, group, sublane):
            token = idx_smem[slot][r] & (N - 1)
            src = x_hbm.at[lax.shift_right_logical(token, 3), token & (V7X_SUBLANES - 1)]
            pltpu.make_async_copy(src, xbuf.at[slot, group, sublane], gather_sem.at[slot]).start()
        for_rows(tile_rows[tile], start)

    def start_scatters(tile, slot):
        def start(r, group, sublane):
            dest = idx_smem[slot][r]
            dst = y_hbm.at[lax.shift_right_logical(dest, 3), dest & (V7X_SUBLANES - 1)]
            pltpu.make_async_copy(obuf.at[slot, group, sublane], dst, scatter_sem.at[slot]).start()
        for_rows(tile_rows[tile], start)

    def wait_rows(n, groups_copy, row_copy):
        n_groups = lax.shift_right_logical(n, 3)

        @pl.when(n_groups > 0)
        def _():
            groups_copy(n_groups).wait()

        def single(r, carry):
            row_copy(r & (V7X_SUBLANES - 1)).wait()
            return carry
        lax.fori_loop(n_groups * V7X_SUBLANES, n, single, 0)

    def wait_gathers(tile, slot):
        sem = gather_sem.at[slot]
        wait_rows(
            tile_rows[tile],
            lambda count: pltpu.make_async_copy(
                x_hbm.at[pl.ds(0, count)], xbuf.at[slot, pl.ds(0, count)], sem),
            lambda k: pltpu.make_async_copy(x_hbm.at[0, k], xbuf.at[slot, 0, k], sem))

    def wait_scatters(tile, slot):
        sem = scatter_sem.at[slot]
        wait_rows(
            tile_rows[tile],
            lambda count: pltpu.make_async_copy(
                obuf.at[slot, pl.ds(0, count)], y_hbm.at[pl.ds(0, count)], sem),
            lambda k: pltpu.make_async_copy(obuf.at[slot, 0, k], y_hbm.at[0, k], sem))

    for slot in range(2):
        other = 1 - slot
        mine = (i & 1) == slot

        @pl.when(mine & (j == 0))
        def _():
            if slot == 0:
                @pl.when(i == 0)
                def _():
                    xbuf[...] = jnp.zeros_like(xbuf)
                    idx_copy(0, 0).start()
                    idx_copy(0, 0).wait()
                    start_gathers(0, 0)

            wait_gathers(i, slot)
            xb[...] = _rms_norm(xbuf[slot].reshape(TM, D), g_ref[...]).astype(_BF16)
            acc[...] = jnp.zeros_like(acc)

            @pl.when(i + 1 < n_tiles)
            def _():
                idx_copy(i + 1, other).start()

        @pl.when(mine & (j == 1) & (i + 1 < n_tiles))
        def _():
            idx_copy(i + 1, other).wait()
            start_gathers(i + 1, other)

    @pl.when(tile_rows[i] > 0)
    def _():
        h = xb[...]
        gate = jnp.dot(h, wg_ref[...], preferred_element_type=_F32)
        up = jnp.dot(h, wu_ref[...], preferred_element_type=_F32)
        act = (gate * jax.nn.sigmoid(gate) * up).astype(_BF16)
        acc[...] += jnp.dot(act, wd_ref[...], preferred_element_type=_F32)

    for slot in range(2):
        other = 1 - slot

        @pl.when(((i & 1) == slot) & (j == n_chunks - 1))
        def _():
            @pl.when(i >= 2)
            def _():
                wait_scatters(i - 2, slot)

            obuf[slot] = acc[...].reshape(obuf.shape[1:])
            start_scatters(i, slot)

            @pl.when(i == n_tiles - 1)
            def _():
                @pl.when(i >= 1)
                def _():
                    wait_scatters(i - 1, other)
                wait_scatters(i, slot)


def _moe_ffn(x2, layer, index, g, router_w_t, wg, wu, wd):
    N, D = x2.shape
    E, F = wg.shape[1], wg.shape[3]
    TM, FC = MOE_TILE, MOE_FF_CHUNK
    assert N & (N - 1) == 0, "token index is recovered from the destination id by masking"
    assert (TOP_K * N) % TM == 0 and F % FC == 0 and F // FC >= 2 and TM % V7X_SUBLANES == 0
    n_tiles = TOP_K * N // TM + E
    n_chunks = F // FC
    row_groups = (TM // V7X_SUBLANES, V7X_SUBLANES, D)

    idx, wts, rank, cnt = _router(x2, layer, g, router_w_t)

    tiles_per = (cnt + TM - 1) // TM
    tile_end = jnp.cumsum(tiles_per)
    tile_start = tile_end - tiles_per
    pos = (tile_start * TM)[idx.reshape(-1)] + rank.reshape(-1)
    dest = jnp.zeros((n_tiles * TM,), jnp.int32).at[pos].set(
        jnp.arange(TOP_K * N, dtype=jnp.int32))
    tile_id = jnp.arange(n_tiles, dtype=jnp.int32)
    used = tile_id < tile_end[-1]
    owner = jnp.minimum(jnp.sum(tile_id[:, None] >= tile_end[None, :], axis=1), E - 1)
    last_owner = owner[jnp.maximum(tile_end[-1] - 1, 0)]
    tile_expert = jnp.where(used, owner, last_owner).astype(jnp.int32)
    tile_rows = jnp.where(used, jnp.clip(cnt[owner] - (tile_id - tile_start[owner]) * TM, 0, TM),
                          0).astype(jnp.int32)

    def chunk_of(i, j, te, tr):
        return jnp.where(tr[i] > 0, j, n_chunks - 1)

    grid_spec = pltpu.PrefetchScalarGridSpec(
        num_scalar_prefetch=2,
        grid=(n_tiles, n_chunks),
        in_specs=[
            pl.BlockSpec(memory_space=pl.ANY),
            pl.BlockSpec(memory_space=pl.ANY),
            pl.BlockSpec((None, 1, D), lambda i, j, te, tr: (layer, 0, 0)),
            pl.BlockSpec((None, None, D, FC),
                         lambda i, j, te, tr: (index, te[i], 0, chunk_of(i, j, te, tr))),
            pl.BlockSpec((None, None, D, FC),
                         lambda i, j, te, tr: (index, te[i], 0, chunk_of(i, j, te, tr))),
            pl.BlockSpec((None, None, FC, D),
                         lambda i, j, te, tr: (index, te[i], chunk_of(i, j, te, tr), 0)),
        ],
        out_specs=pl.BlockSpec(memory_space=pl.ANY),
        scratch_shapes=[
            pltpu.SMEM((TM,), jnp.int32),
            pltpu.SMEM((TM,), jnp.int32),
            pltpu.VMEM((2,) + row_groups, _F32),
            pltpu.VMEM((TM, D), _BF16),
            pltpu.VMEM((TM, D), _F32),
            pltpu.VMEM((2,) + row_groups, _F32),
            pltpu.SemaphoreType.DMA((2,)),
            pltpu.SemaphoreType.DMA((2,)),
            pltpu.SemaphoreType.DMA((2,)),
        ],
    )
    y = pl.pallas_call(
        _moe_kernel,
        grid_spec=grid_spec,
        out_shape=jax.ShapeDtypeStruct((TOP_K * N // V7X_SUBLANES, V7X_SUBLANES, D), _F32),
        compiler_params=pltpu.CompilerParams(
            dimension_semantics=("arbitrary", "arbitrary"),
            vmem_limit_bytes=V7X_VMEM_LIMIT_BYTES),
        name="moe_ffn",
    )(tile_expert, tile_rows, dest.reshape(n_tiles, TM),
      x2.reshape(N // V7X_SUBLANES, V7X_SUBLANES, D), g, wg, wu, wd)
    return y.reshape(TOP_K * N, D), wts.T


def _final_norm_kernel(x_ref, y0_ref, y1_ref, yw_ref, g_ref, o_ref):
    yw = yw_ref[...]
    x = x_ref[...] + yw[:, 0:1] * y0_ref[...] + yw[:, 1:2] * y1_ref[...]
    o_ref[...] = _rms_norm(x, g_ref[...])


def _final_norm(x2, y, yw, g):
    N, D = x2.shape
    TM = DENSE_TILE
    n_blocks = N // TM
    row_spec = pl.BlockSpec((TM, D), lambda i: (i, 0))
    return pl.pallas_call(
        _final_norm_kernel,
        grid=(n_blocks,),
        in_specs=[row_spec, row_spec, pl.BlockSpec((TM, D), lambda i: (n_blocks + i, 0)),
                  pl.BlockSpec((TM, TOP_K), lambda i: (i, 0)), _const_spec((1, D))],
        out_specs=row_spec,
        out_shape=jax.ShapeDtypeStruct((N, D), _F32),
        compiler_params=pltpu.CompilerParams(dimension_semantics=("arbitrary",)),
        name="final_norm",
    )(x2, y, y, yw, g.reshape(1, D))


def kernel(x, mix_norm, w_in, conv_dw, conv_dw_b, conv_ln_g, conv_ln_b, w_conv_out, w_pool,
           pool_scale, w_out, ffn_norm, dense_w_gate, dense_w_up, dense_w_down, router_w,
           moe_w_gate, moe_w_up, moe_w_down, final_norm):
    B, S, D = x.shape
    N = B * S
    depth = mix_norm.shape[0]
    C = conv_dw.shape[-1]
    bf = lambda w: w.astype(_BF16)
    row = lambda p: p.reshape(p.shape[0], 1, p.shape[1])
    mixer_params = (row(mix_norm), bf(w_in),
                    jnp.broadcast_to(conv_dw[:, :, None, :], (depth, CONV_WIDTH, V7X_SUBLANES, C)),
                    row(conv_dw_b), row(conv_ln_g), row(conv_ln_b), bf(w_conv_out), bf(w_pool),
                    row(pool_scale), bf(w_out))
    ffn_g = row(ffn_norm)
    dense_w = (bf(dense_w_gate), bf(dense_w_up), bf(dense_w_down))
    moe_w = (bf(moe_w_gate), bf(moe_w_up), bf(moe_w_down))
    router_w_t = jnp.swapaxes(router_w, 1, 2)

    moe_out = None
    for layer in range(depth):
        moe_in = None
        if moe_out is not None:
            y, yw = moe_out
            moe_in = (y.reshape(TOP_K, B, S, D), yw.reshape(B, S, TOP_K))
        x = _mixer(x, moe_in, layer, *mixer_params)
        moe_out = None
        x2 = x.reshape(N, D)
        index = layer // 2
        if layer % 2 == 0:
            x = _dense_ffn(x2, layer, index, ffn_g, *dense_w).reshape(B, S, D)
        else:
            moe_out = _moe_ffn(x2, layer, index, ffn_g, router_w_t[index], *moe_w)
    x2 = x.reshape(N, D)
    if moe_out is None:
        moe_out = (jnp.zeros((TOP_K * N, D), _F32), jnp.zeros((N, TOP_K), _F32))
    return _final_norm(x2, *moe_out, final_norm).reshape(B, S, D)
```

```python
import functools

import jax
import jax.numpy as jnp
from jax import lax
from jax.experimental import pallas as pl
from jax.experimental.pallas import tpu as pltpu

RMS_EPS = 1e-6
LN_EPS = 1e-5
CONV_WIDTH = 31
POOL_WINDOWS = (2, 4, 8, 16)
N_EXPERTS = 8
TOP_K = 2

V7X_SUBLANES = 8
V7X_LANES = 128
V7X_VMEM_LIMIT_BYTES = 56 * 1024 * 1024

CONV_HALO = 32
POOL_HALO = 16
ROW_INTERLEAVE = 4
ROW_SPAN = V7X_SUBLANES * ROW_INTERLEAVE
CONV_ROW_BLOCK = 128

MIX_TILE = 512
DENSE_TILE = 512
DENSE_FF_CHUNK = 512
ROUTER_TILE = 1024
MOE_TILE = 1024
MOE_FF_CHUNK = 512
MOE_ROW_CHUNK = 128
MOE_INDEX_SLOTS = 4
MOE_DOWN_PIECES = 2
MOE_STAGE_ROWS = (96, 32)

_BF16 = jnp.bfloat16
_F32 = jnp.float32


def _rms_norm(x, g):
    return x * lax.rsqrt(jnp.mean(x * x, axis=-1, keepdims=True) + RMS_EPS) * g


def _const_spec(shape, index=None):
    index = (0,) * len(shape) if index is None else index
    return pl.BlockSpec(shape, lambda *_: index, pipeline_mode=pl.Buffered(1))


def _layer_spec(stacked, layer):
    return _const_spec((None,) + stacked.shape[1:], (layer,) + (0,) * (stacked.ndim - 1))


def _interleaved_rows(start):
    return pl.ds(start, V7X_SUBLANES, stride=ROW_INTERLEAVE)


def _interleaved_starts(r0, rows):
    return [r0 + b + o for b in range(0, rows, ROW_SPAN) for o in range(ROW_INTERLEAVE)]


def _mixer_kernel(has_moe_in, *refs):
    if has_moe_in:
        x_ref, y0_ref, y1_ref, yw_ref = refs[:4]
        refs = refs[4:]
    else:
        x_ref = refs[0]
        refs = refs[1:]
    (g_ref, w_in_ref, dw_ref, dwb_ref, lng_ref, lnb_ref, wco_ref, wpool_ref, pscale_ref,
     wout_ref, o_ref, ubuf, pbuf, cbuf, wbuf) = refs
    T, D = x_ref.shape
    n_slab = ubuf.shape[0]
    C = n_slab * V7X_LANES
    s = pl.program_id(1)

    @pl.when(s == 0)
    def _():
        ubuf[:, 0:CONV_HALO, :] = jnp.zeros((n_slab, CONV_HALO, V7X_LANES), _F32)
        pbuf[:, 0:POOL_HALO, :] = jnp.zeros((n_slab, POOL_HALO, V7X_LANES), _F32)

    x = x_ref[...]
    if has_moe_in:
        yw = yw_ref[...]
        x = x + yw[:, 0:1] * y0_ref[...] + yw[:, 1:2] * y1_ref[...]
    hb = _rms_norm(x, g_ref[...]).astype(_BF16)

    def proj(c0, width):
        return jnp.dot(hb, w_in_ref[:, c0:c0 + width], preferred_element_type=_F32)

    def lanes(c):
        return slice(c * V7X_LANES, (c + 1) * V7X_LANES)

    u = proj(0, C) * jax.nn.sigmoid(proj(C, C))
    for c in range(n_slab):
        ubuf[c, CONV_HALO:CONV_HALO + T, :] = u[:, lanes(c)]
    p = proj(2 * C, C)
    zc = proj(3 * C, D)
    zp = proj(3 * C + D, D)
    for c in range(n_slab):
        for r0 in range(0, T, CONV_ROW_BLOCK):
            starts = _interleaved_starts(r0, CONV_ROW_BLOCK)
            accs = [None] * len(starts)
            for k in range(CONV_WIDTH):
                tap = dw_ref[k, :, lanes(c)]
                shift = CONV_HALO - (CONV_WIDTH - 1) + k
                for q, r in enumerate(starts):
                    term = tap * ubuf[c, _interleaved_rows(shift + r), :]
                    accs[q] = term if accs[q] is None else accs[q] + term
            for q, r in enumerate(starts):
                cbuf[c, _interleaved_rows(r), :] = accs[q]
    ubuf[:, 0:CONV_HALO, :] = ubuf[:, T:T + CONV_HALO, :]
    cv = jnp.concatenate([cbuf[c] for c in range(n_slab)], axis=-1) + dwb_ref[...]
    mu = jnp.mean(cv, axis=-1, keepdims=True)
    cc = cv - mu
    var = jnp.mean(cc * cc, axis=-1, keepdims=True)
    cn = cc * lax.rsqrt(var + LN_EPS) * lng_ref[...] + lnb_ref[...]
    cn = cn * jax.nn.sigmoid(cn)
    yc = jnp.dot(cn.astype(_BF16), wco_ref[...], preferred_element_type=_F32)

    for c in range(n_slab):
        pbuf[c, POOL_HALO:POOL_HALO + T, :] = p[:, lanes(c)]
    for c, w in enumerate(POOL_WINDOWS):
        for r in _interleaved_starts(0, T):
            win = pbuf[c, _interleaved_rows(POOL_HALO + r), :]
            for jj in range(1, w):
                win = win + pbuf[c, _interleaved_rows(POOL_HALO + r - jj), :]
            wbuf[c, _interleaved_rows(r), :] = win
    pbuf[:, 0:POOL_HALO, :] = pbuf[:, T:T + POOL_HALO, :]
    pos = s * T + lax.broadcasted_iota(jnp.int32, (T, 1), 0)
    yps = []
    for c, w in enumerate(POOL_WINDOWS):
        count = jnp.minimum(pos + 1, w).astype(_F32)
        delta = wbuf[c] / count - p[:, lanes(c)]
        yps.append(jnp.dot(delta.astype(_BF16), wpool_ref[c], preferred_element_type=_F32))
    yp = jnp.concatenate(yps, axis=-1) * pscale_ref[...]

    merged = jax.nn.sigmoid(zc) * yc + jax.nn.sigmoid(zp) * yp
    o_ref[...] = x + jnp.dot(merged.astype(_BF16), wout_ref[...], preferred_element_type=_F32)


def _mixer(x, moe_in, layer, g, w_in, dw8, dwb, lng, lnb, wco, wpool, pscale, wout):
    B, S, D = x.shape
    C = dw8.shape[-1]
    T = MIX_TILE
    n_slab = C // V7X_LANES
    assert S % T == 0 and T % CONV_ROW_BLOCK == 0 and CONV_ROW_BLOCK % ROW_SPAN == 0
    assert CONV_HALO >= CONV_WIDTH - 1 and POOL_HALO >= max(POOL_WINDOWS) - 1 and T >= CONV_HALO
    assert n_slab == len(POOL_WINDOWS) and D % len(POOL_WINDOWS) == 0
    assert w_in.shape[1:] == (D, 3 * C + 2 * D) and wpool.shape[1:] == (n_slab, V7X_LANES, D // n_slab)
    row_spec = pl.BlockSpec((None, T, D), lambda b, s: (b, s, 0))
    has_moe_in = moe_in is not None
    acts, act_specs = [x], [row_spec]
    if has_moe_in:
        y, yw = moe_in
        n_seq, n_tok = S // T, B * S // T
        acts += [y, y, yw]
        act_specs += [pl.BlockSpec((T, D), lambda b, s: (b * n_seq + s, 0)),
                      pl.BlockSpec((T, D), lambda b, s: (n_tok + b * n_seq + s, 0)),
                      pl.BlockSpec((None, T, TOP_K), lambda b, s: (b, s, 0))]
    params = (g, w_in, dw8, dwb, lng, lnb, wco, wpool, pscale, wout)
    return pl.pallas_call(
        functools.partial(_mixer_kernel, has_moe_in),
        grid=(B, S // T),
        in_specs=act_specs + [_layer_spec(p, layer) for p in params],
        out_specs=row_spec,
        out_shape=jax.ShapeDtypeStruct((B, S, D), _F32),
        scratch_shapes=[
            pltpu.VMEM((n_slab, CONV_HALO + T, V7X_LANES), _F32),
            pltpu.VMEM((n_slab, POOL_HALO + T, V7X_LANES), _F32),
            pltpu.VMEM((n_slab, T, V7X_LANES), _F32),
            pltpu.VMEM((n_slab, T, V7X_LANES), _F32),
        ],
        compiler_params=pltpu.CompilerParams(
            dimension_semantics=("arbitrary", "arbitrary"),
            vmem_limit_bytes=V7X_VMEM_LIMIT_BYTES),
        name="mixer",
    )(*acts, *params)


def _ff_chunks(total, chunk):
    return [(c0, min(chunk, total - c0)) for c0 in range(0, total, chunk)]


def _dense_ffn_kernel(x_ref, g_ref, wg_ref, wu_ref, wd_ref, o_ref):
    x = x_ref[...]
    hb = _rms_norm(x, g_ref[...]).astype(_BF16)
    acc = x
    for c0, cw in _ff_chunks(wg_ref.shape[1], DENSE_FF_CHUNK):
        gate = jnp.dot(hb, wg_ref[:, c0:c0 + cw], preferred_element_type=_F32)
        up = jnp.dot(hb, wu_ref[:, c0:c0 + cw], preferred_element_type=_F32)
        act = (gate * jax.nn.sigmoid(gate) * up).astype(_BF16)
        acc = acc + jnp.dot(act, wd_ref[c0:c0 + cw, :], preferred_element_type=_F32)
    o_ref[...] = acc


def _dense_ffn(x2, layer, index, g, wg, wu, wd):
    N, D = x2.shape
    TM = DENSE_TILE
    assert N % TM == 0
    row_spec = pl.BlockSpec((TM, D), lambda i: (i, 0))
    return pl.pallas_call(
        _dense_ffn_kernel,
        grid=(N // TM,),
        in_specs=[row_spec, _layer_spec(g, layer)] + [_layer_spec(w, index) for w in (wg, wu, wd)],
        out_specs=row_spec,
        out_shape=jax.ShapeDtypeStruct((N, D), _F32),
        compiler_params=pltpu.CompilerParams(
            dimension_semantics=("arbitrary",), vmem_limit_bytes=V7X_VMEM_LIMIT_BYTES),
        name="dense_ffn",
    )(x2, g, wg, wu, wd)


def _router_kernel(x_ref, g_ref, rw_ref, tri_ref, idx_ref, wts_ref, rank_ref, cnt_ref, carry):
    i = pl.program_id(0)

    @pl.when(i == 0)
    def _():
        carry[...] = jnp.zeros_like(carry)

    h = _rms_norm(x_ref[...], g_ref[...])
    logits = lax.dot_general(rw_ref[...], h, (((1,), (1,)), ((), ())),
                             precision=lax.Precision.HIGHEST,
                             preferred_element_type=_F32)
    E, TR = logits.shape
    eid = lax.broadcasted_iota(jnp.int32, (E, TR), 0)
    m1 = jnp.max(logits, axis=0, keepdims=True)
    i1 = jnp.min(jnp.where(logits == m1, eid, E), axis=0, keepdims=True)
    rest = jnp.where(eid == i1, -jnp.inf, logits)
    m2 = jnp.max(rest, axis=0, keepdims=True)
    i2 = jnp.min(jnp.where(rest == m2, eid, E), axis=0, keepdims=True)
    e2 = jnp.exp(m2 - m1)
    denom = 1.0 + e2
    idx_ref[...] = jnp.concatenate([i1, i2], axis=0)
    wts_ref[...] = jnp.concatenate([1.0 / denom, e2 / denom], axis=0)

    sel1 = eid == i1
    sel2 = eid == i2
    onehot = jnp.where(sel1 | sel2, 1.0, 0.0)
    before = jnp.dot(onehot.astype(_BF16), tri_ref[...], preferred_element_type=_F32) + carry[...]
    r1 = jnp.sum(jnp.where(sel1, before, 0.0), axis=0, keepdims=True)
    r2 = jnp.sum(jnp.where(sel2, before, 0.0), axis=0, keepdims=True)
    rank_ref[...] = jnp.concatenate([r1, r2], axis=0).astype(jnp.int32)
    total = carry[...] + jnp.sum(onehot, axis=1, keepdims=True)
    carry[...] = total
    cnt_ref[...] = jnp.broadcast_to(total, cnt_ref.shape)


def _router(x2, layer, g, router_w_t):
    N, D = x2.shape
    E = router_w_t.shape[0]
    TR = ROUTER_TILE
    assert N % TR == 0 and E == N_EXPERTS
    tri = (lax.broadcasted_iota(jnp.int32, (TR, TR), 0)
           < lax.broadcasted_iota(jnp.int32, (TR, TR), 1)).astype(_BF16)
    lane_spec = pl.BlockSpec((TOP_K, TR), lambda i: (0, i))
    idx, wts, rank, cnt = pl.pallas_call(
        _router_kernel,
        grid=(N // TR,),
        in_specs=[pl.BlockSpec((TR, D), lambda i: (i, 0)),
                  _layer_spec(g, layer), _const_spec((E, D)), _const_spec((TR, TR))],
        out_specs=[lane_spec, lane_spec, lane_spec,
                   pl.BlockSpec((E, V7X_LANES), lambda i: (0, 0))],
        out_shape=[jax.ShapeDtypeStruct((TOP_K, N), jnp.int32),
                   jax.ShapeDtypeStruct((TOP_K, N), _F32),
                   jax.ShapeDtypeStruct((TOP_K, N), jnp.int32),
                   jax.ShapeDtypeStruct((E, V7X_LANES), _F32)],
        scratch_shapes=[pltpu.VMEM((E, 1), _F32)],
        compiler_params=pltpu.CompilerParams(
            dimension_semantics=("arbitrary",), vmem_limit_bytes=V7X_VMEM_LIMIT_BYTES),
        name="router",
    )(x2, g, router_w_t, tri)
    return idx, wts, rank, cnt[:, 0].astype(jnp.int32)


def _moe_kernel(n_steps,
                tile_expert, tile_rows,
                dest_hbm, x_hbm, g_ref, wg_ref, wu_ref, wd_ref,
                y_hbm,
                idx_smem, rows, xb, idx_sem, gather_sem, scatter_sem):
    i = pl.program_id(0)
    j = pl.program_id(1)
    n_tiles = pl.num_programs(0)
    n_chunks = pl.num_programs(1)
    N = x_hbm.shape[0] * V7X_SUBLANES
    TM, D = xb.shape
    n_groups = TM // V7X_SUBLANES
    xbuf = rows.at[pl.ds(0, 2)]
    acc = rows.at[2]
    obuf = rows.at[pl.ds(3, 2)]
    step_rows = MOE_ROW_CHUNK * n_steps

    def ring_offset(q):
        return pl.multiple_of((q & (MOE_INDEX_SLOTS - 1)) * TM, TM)

    def idx_copy(q):
        slot = q & (MOE_INDEX_SLOTS - 1)
        return pltpu.make_async_copy(dest_hbm.at[q], idx_smem.at[pl.ds(ring_offset(q), TM)],
                                     idx_sem.at[slot])

    def gather_row(ids_at, slot, group, sublane):
        token = idx_smem[ids_at] & (N - 1)
        src = x_hbm.at[lax.shift_right_logical(token, 3), token & (V7X_SUBLANES - 1)]
        pltpu.make_async_copy(src, xbuf.at[slot, group, sublane], gather_sem.at[slot]).start()

    def scatter_row(ids_at, slot, group, sublane):
        dest = idx_smem[ids_at]
        dst = y_hbm.at[lax.shift_right_logical(dest, 3), dest & (V7X_SUBLANES - 1)]
        pltpu.make_async_copy(obuf.at[slot, group, sublane], dst, scatter_sem.at[slot]).start()

    def move_rows(slot, first_group, n_rows):
        gather_ids = ring_offset(i + 2) + first_group * V7X_SUBLANES
        scatter_ids = ring_offset(i) + first_group * V7X_SUBLANES
        for k in range(n_rows):
            group, sublane = first_group + k // V7X_SUBLANES, k % V7X_SUBLANES
            gather_row(gather_ids + k, slot, group, sublane)
            scatter_row(scatter_ids + k, slot, group, sublane)

    def for_all_rows(q, fn, slot):
        def body(group, carry):
            for sublane in range(V7X_SUBLANES):
                fn(ring_offset(q) + group * V7X_SUBLANES + sublane, slot, group, sublane)
            return carry
        lax.fori_loop(0, n_groups, body, 0)

    def wait_gathers(slot):
        pltpu.make_async_copy(x_hbm.at[pl.ds(0, n_groups)], xbuf.at[slot],
                              gather_sem.at[slot]).wait()

    def wait_scatters(slot):
        pltpu.make_async_copy(obuf.at[slot], y_hbm.at[pl.ds(0, n_groups)],
                              scatter_sem.at[slot]).wait()

    used = tile_rows[i] > 0
    for slot in range(2):
        other = 1 - slot
        mine = (i & 1) == slot

        @pl.when(mine & (j == 0))
        def _():
            if slot == 0:
                @pl.when(i == 0)
                def _():
                    obuf[...] = jnp.zeros(obuf.shape, _F32)
                    for q in range(3):
                        idx_copy(q).start()
                    for q in range(3):
                        idx_copy(q).wait()
                    for_all_rows(1, gather_row, 0)

            @pl.when(i + 3 < n_tiles + 2)
            def _():
                idx_copy(i + 3).start()

            @pl.when(i >= 1)
            def _():
                idx_copy(i + 2).wait()

            wait_gathers(slot)
            xb[...] = _rms_norm(xbuf[slot].reshape(TM, D), g_ref[...]).astype(_BF16)
            acc[...] = jnp.zeros(acc.shape, _F32)
            move_rows(other, step_rows // V7X_SUBLANES, TM - step_rows)

        @pl.when(mine & used)
        def _():
            h = xb[...]
            gate = jnp.dot(h, wg_ref[...], preferred_element_type=_F32)
            up = jnp.dot(h, wu_ref[...], preferred_element_type=_F32)
            act = (gate * jax.nn.sigmoid(gate) * up).astype(_BF16)
            piece_cols = D // MOE_DOWN_PIECES
            first_group = j * (MOE_ROW_CHUNK // V7X_SUBLANES)
            for p, n_rows in enumerate(MOE_STAGE_ROWS):
                move_rows(other, first_group, n_rows)
                first_group = first_group + n_rows // V7X_SUBLANES
                cols = slice(p * piece_cols, (p + 1) * piece_cols)
                part = jnp.dot(act, wd_ref[:, cols], preferred_element_type=_F32)
                acc[:, :, cols] += part.reshape(n_groups, V7X_SUBLANES, piece_cols)

        @pl.when(mine & jnp.logical_not(used))
        def _():
            move_rows(other, j * (MOE_ROW_CHUNK // V7X_SUBLANES), MOE_ROW_CHUNK)

        @pl.when(mine & (j == n_chunks - 1))
        def _():
            @pl.when(i >= 1)
            def _():
                wait_scatters(slot)

            obuf[slot] = acc[...]

            @pl.when(i == n_tiles - 1)
            def _():
                for_all_rows(i + 1, scatter_row, slot)
                wait_scatters(other)
                wait_scatters(slot)
                wait_gathers(other)


def _moe_ffn(x2, layer, index, g, router_w_t, wg, wu, wd):
    N, D = x2.shape
    E, F = wg.shape[1], wg.shape[3]
    TM, FC = MOE_TILE, MOE_FF_CHUNK
    n_tiles = TOP_K * N // TM + E
    n_chunks = F // FC
    row_groups = (TM // V7X_SUBLANES, V7X_SUBLANES, D)
    n_spare = 2 * TM
    assert N & (N - 1) == 0, "token index is recovered from the destination id by masking"
    assert (TOP_K * N) % TM == 0 and F % FC == 0 and n_spare <= N and n_tiles >= 2
    assert MOE_ROW_CHUNK % V7X_SUBLANES == 0 and MOE_ROW_CHUNK * n_chunks <= TM
    assert TM % V7X_SUBLANES == 0 and MOE_INDEX_SLOTS == 4
    assert (TM - MOE_ROW_CHUNK * n_chunks) % V7X_SUBLANES == 0
    assert D % MOE_DOWN_PIECES == 0 and len(MOE_STAGE_ROWS) == MOE_DOWN_PIECES
    assert sum(MOE_STAGE_ROWS) == MOE_ROW_CHUNK and all(r % V7X_SUBLANES == 0 for r in MOE_STAGE_ROWS)

    idx, wts, rank, cnt = _router(x2, layer, g, router_w_t)

    tiles_per = (cnt + TM - 1) // TM
    tile_end = jnp.cumsum(tiles_per)
    tile_start = tile_end - tiles_per
    pos = (tile_start * TM)[idx.reshape(-1)] + rank.reshape(-1)
    table_row = jnp.arange(n_tiles + 2, dtype=jnp.int32)[:, None]
    spare = TOP_K * N + (table_row & 1) * TM + jnp.arange(TM, dtype=jnp.int32)[None, :]
    dest = spare.reshape(-1).at[pos + TM].set(jnp.arange(TOP_K * N, dtype=jnp.int32))
    tile_id = jnp.arange(n_tiles, dtype=jnp.int32)
    used = tile_id < tile_end[-1]
    owner = jnp.minimum(jnp.sum(tile_id[:, None] >= tile_end[None, :], axis=1), E - 1)
    last_owner = owner[jnp.maximum(tile_end[-1] - 1, 0)]
    tile_expert = jnp.where(used, owner, last_owner).astype(jnp.int32)
    tile_rows = jnp.where(used, jnp.clip(cnt[owner] - (tile_id - tile_start[owner]) * TM, 0, TM),
                          0).astype(jnp.int32)

    def chunk_of(i, j, te, tr):
        return jnp.where(tr[i] > 0, j, n_chunks - 1)

    grid_spec = pltpu.PrefetchScalarGridSpec(
        num_scalar_prefetch=2,
        grid=(n_tiles, n_chunks),
        in_specs=[
            pl.BlockSpec(memory_space=pl.ANY),
            pl.BlockSpec(memory_space=pl.ANY),
            pl.BlockSpec((None, 1, D), lambda i, j, te, tr: (layer, 0, 0)),
            pl.BlockSpec((None, None, D, FC),
                         lambda i, j, te, tr: (index, te[i], 0, chunk_of(i, j, te, tr))),
            pl.BlockSpec((None, None, D, FC),
                         lambda i, j, te, tr: (index, te[i], 0, chunk_of(i, j, te, tr))),
            pl.BlockSpec((None, None, FC, D),
                         lambda i, j, te, tr: (index, te[i], chunk_of(i, j, te, tr), 0)),
        ],
        out_specs=pl.BlockSpec(memory_space=pl.ANY),
        scratch_shapes=[
            pltpu.SMEM((MOE_INDEX_SLOTS * TM,), jnp.int32),
            pltpu.VMEM((5,) + row_groups, _F32),
            pltpu.VMEM((TM, D), _BF16),
            pltpu.SemaphoreType.DMA((MOE_INDEX_SLOTS,)),
            pltpu.SemaphoreType.DMA((2,)),
            pltpu.SemaphoreType.DMA((2,)),
        ],
    )
    n_y = TOP_K * N + n_spare
    y = pl.pallas_call(
        functools.partial(_moe_kernel, n_chunks),
        grid_spec=grid_spec,
        out_shape=jax.ShapeDtypeStruct((n_y // V7X_SUBLANES, V7X_SUBLANES, D), _F32),
        compiler_params=pltpu.CompilerParams(
            dimension_semantics=("arbitrary", "arbitrary"),
            vmem_limit_bytes=V7X_VMEM_LIMIT_BYTES),
        name="moe_ffn",
    )(tile_expert, tile_rows, dest.reshape(n_tiles + 2, TM),
      x2.reshape(N // V7X_SUBLANES, V7X_SUBLANES, D), g, wg, wu, wd)
    return y.reshape(n_y, D), wts.T


def _final_norm_kernel(x_ref, y0_ref, y1_ref, yw_ref, g_ref, o_ref):
    yw = yw_ref[...]
    x = x_ref[...] + yw[:, 0:1] * y0_ref[...] + yw[:, 1:2] * y1_ref[...]
    o_ref[...] = _rms_norm(x, g_ref[...])


def _final_norm(x2, y, yw, g):
    N, D = x2.shape
    TM = DENSE_TILE
    n_blocks = N // TM
    row_spec = pl.BlockSpec((TM, D), lambda i: (i, 0))
    return pl.pallas_call(
        _final_norm_kernel,
        grid=(n_blocks,),
        in_specs=[row_spec, row_spec, pl.BlockSpec((TM, D), lambda i: (n_blocks + i, 0)),
                  pl.BlockSpec((TM, TOP_K), lambda i: (i, 0)), _const_spec((1, D))],
        out_specs=row_spec,
        out_shape=jax.ShapeDtypeStruct((N, D), _F32),
        compiler_params=pltpu.CompilerParams(dimension_semantics=("arbitrary",)),
        name="final_norm",
    )(x2, y, y, yw, g.reshape(1, D))


def kernel(x, mix_norm, w_in, conv_dw, conv_dw_b, conv_ln_g, conv_ln_b, w_conv_out, w_pool,
           pool_scale, w_out, ffn_norm, dense_w_gate, dense_w_up, dense_w_down, router_w,
           moe_w_gate, moe_w_up, moe_w_down, final_norm):
    B, S, D = x.shape
    N = B * S
    depth = mix_norm.shape[0]
    C = conv_dw.shape[-1]
    bf = lambda w: w.astype(_BF16)
    row = lambda p: p.reshape(p.shape[0], 1, p.shape[1])
    mixer_params = (row(mix_norm), bf(w_in),
                    jnp.broadcast_to(conv_dw[:, :, None, :], (depth, CONV_WIDTH, V7X_SUBLANES, C)),
                    row(conv_dw_b), row(conv_ln_g), row(conv_ln_b), bf(w_conv_out), bf(w_pool),
                    row(pool_scale), bf(w_out))
    ffn_g = row(ffn_norm)
    dense_w = (bf(dense_w_gate), bf(dense_w_up), bf(dense_w_down))
    moe_w = (bf(moe_w_gate), bf(moe_w_up), bf(moe_w_down))
    router_w_t = jnp.swapaxes(router_w, 1, 2)

    moe_out = None
    for layer in range(depth):
        moe_in = None
        if moe_out is not None:
            y, yw = moe_out
            moe_in = (y, yw.reshape(B, S, TOP_K))
        x = _mixer(x, moe_in, layer, *mixer_params)
        moe_out = None
        x2 = x.reshape(N, D)
        index = layer // 2
        if layer % 2 == 0:
            x = _dense_ffn(x2, layer, index, ffn_g, *dense_w).reshape(B, S, D)
        else:
            moe_out = _moe_ffn(x2, layer, index, ffn_g, router_w_t[index], *moe_w)
    x2 = x.reshape(N, D)
    if moe_out is None:
        moe_out = (jnp.zeros((TOP_K * N, D), _F32), jnp.zeros((N, TOP_K), _F32))
    return _final_norm(x2, *moe_out, final_norm).reshape(B, S, D)
```
